```python
import math
import jax, jax.numpy as jnp
from jax import lax
import numpy as np

D_MODEL = 1024
BATCH = 8
SEQ = 2048
DEPTH = 2
DEC_BATCH = 32
DEC_SEQ = 1
PAST_LEN = 16384
PAGE_SIZE = 128

N_HEADS = 8
N_KV = 2
HPG = N_HEADS // N_KV
HEAD_DIM = 64
CMP_BLOCK = 32
CMP_STRIDE = 16
CMP_HIDDEN = 64
SEL_BLOCK = 64
N_SEL = 16
WINDOW = 512
QBLOCK = 64
D_A = D_MODEL // 4
CONV_A = 3
D_C = D_MODEL // 4
CONV_C = 31
N_BUCKETS = 32
MAX_DIST = 128
N_MEM = 256
X_HEADS = 4
X_HEAD_DIM = 128
D_FF = -(-8 * D_MODEL // (3 * 256)) * 256

Q_W = N_HEADS * HEAD_DIM
KV_W = 6 * N_KV * HEAD_DIM
NG_W = 3 * N_HEADS
A_W = 3 * D_A
C_W = 2 * D_C
MG_W = 3 * D_MODEL
P_IN = Q_W + KV_W + NG_W + A_W + C_W + MG_W
SPLITS = (Q_W, Q_W + KV_W, Q_W + KV_W + NG_W, Q_W + KV_W + NG_W + A_W, Q_W + KV_W + NG_W + A_W + C_W)

NEG_INF = -1e30
SEL_FORCE = 1e6
SEL_NEG = -1e9
RMS_EPS = 1e-6
LN_EPS = 1e-5

kernel_name = 'hybrid_nsa_shortconv_conformer_decoder_step'


def rms_norm(x, g):
    x32 = x.astype(jnp.float32)
    y = x32 * lax.rsqrt(jnp.mean(x32 * x32, axis=-1, keepdims=True) + RMS_EPS)
    return (y * g.astype(jnp.float32)).astype(x.dtype)


def layer_norm(x, g, b):
    x32 = x.astype(jnp.float32)
    mu = jnp.mean(x32, axis=-1, keepdims=True)
    xc = x32 - mu
    var = jnp.mean(xc * xc, axis=-1, keepdims=True)
    return (xc * lax.rsqrt(var + LN_EPS) * g.astype(jnp.float32) + b.astype(jnp.float32)).astype(x.dtype)


def rel_bucket(dist):
    n = jnp.maximum(dist, 0)
    exact = N_BUCKETS // 2
    large = exact + (jnp.log(jnp.maximum(n, exact).astype(jnp.float32) / exact)
                     / math.log(MAX_DIST / exact) * (N_BUCKETS - exact)).astype(jnp.int32)
    return jnp.where(n < exact, n, jnp.minimum(large, N_BUCKETS - 1))


def head_bias(rel_bias, dist):
    tq, nk = dist.shape
    b = rel_bias[rel_bucket(dist)].reshape(tq, nk, N_KV, HPG)
    return b.transpose(2, 3, 0, 1).astype(jnp.float32)


def masked_softmax(logits, mask):
    p = jax.nn.softmax(jnp.where(mask, logits.astype(jnp.float32), NEG_INF), axis=-1)
    return p * mask


def causal_dwconv(u, hist, w):
    width = w.shape[0]
    full = jnp.concatenate([hist, u], axis=1)
    y = lax.conv_general_dilated(full, w[:, None, :], window_strides=(1,), padding='VALID',
                                 dimension_numbers=('NWC', 'WIO', 'NWC'),
                                 feature_group_count=u.shape[-1])
    return y, full[:, -(width - 1):]


def compress_blocks(k, w1, b1, w2, b2, pe):
    B, L, G, D = k.shape
    nch = -(-L // CMP_STRIDE)
    k = jnp.pad(k, ((0, 0), (0, nch * CMP_STRIDE - L), (0, 0), (0, 0)))
    ch = k.reshape(B, nch, CMP_STRIDE, G, D).transpose(0, 1, 3, 2, 4).reshape(B, nch, G, CMP_STRIDE * D)
    half = CMP_STRIDE * D
    h = (jnp.einsum('bngf,fh->bngh', ch[:, :-1], w1[:half])
         + jnp.einsum('bngf,fh->bngh', ch[:, 1:], w1[half:])
         + (pe.reshape(-1) @ w1 + b1))
    return jnp.einsum('bngh,hd->bngd', jax.nn.silu(h), w2) + b2


def nsa_attention(q, k_cmp, v_cmp, k_sel, v_sel, k_win, v_win, gates, q_start, w_start,
                  rel_bias, cmp_w1, cmp_b1, cmp_w2, cmp_b2, cmp_pe, chunked):
    B, T = q.shape[:2]
    L = k_cmp.shape[1]
    scale = HEAD_DIM ** -0.5
    qg = q.reshape(B, T, N_KV, HPG, HEAD_DIM)
    qpos = q_start + jnp.arange(T, dtype=jnp.int32)

    ck = compress_blocks(k_cmp, cmp_w1[0], cmp_b1[0], cmp_w2[0], cmp_b2[0], cmp_pe[0])
    cv = compress_blocks(v_cmp, cmp_w1[1], cmp_b1[1], cmp_w2[1], cmp_b2[1], cmp_pe[1])
    nb = ck.shape[1]
    bend = jnp.arange(nb, dtype=jnp.int32) * CMP_STRIDE + (CMP_BLOCK - 1)
    dist_c = qpos[:, None] - bend[None, :]
    logits_c = jnp.einsum('btghd,bngd->bghtn', qg, ck).astype(jnp.float32) * scale + head_bias(rel_bias, dist_c)
    p_c = masked_softmax(logits_c, dist_c >= 0)
    o_c = jnp.einsum('bghtn,bngd->btghd', p_c.astype(cv.dtype), cv)

    ns = -(-L // SEL_BLOCK)
    cs = jnp.arange(nb, dtype=jnp.int32) * CMP_STRIDE
    ss = jnp.arange(ns, dtype=jnp.int32) * SEL_BLOCK
    overlap = ((cs[:, None] < ss[None, :] + SEL_BLOCK) & (cs[:, None] + CMP_BLOCK > ss[None, :])).astype(jnp.float32)
    p_s = jnp.einsum('bghtn,ns->bgts', p_c, overlap)
    jq = qpos // SEL_BLOCK
    jb = jnp.arange(ns, dtype=jnp.int32)
    future = jb[None, :] > jq[:, None]
    forced = (jb[None, :] == 0) | (jb[None, :] == jq[:, None]) | (jb[None, :] == jq[:, None] - 1)
    score = jnp.where(future, SEL_NEG, jnp.where(forced, SEL_FORCE, p_s))
    _, idx = lax.top_k(score, min(N_SEL, ns))

    pad = ns * SEL_BLOCK - L

    def to_blocks(a):
        a = jnp.pad(a, ((0, 0), (0, pad), (0, 0), (0, 0)))
        return a.reshape(B, ns, SEL_BLOCK, N_KV, HEAD_DIM).transpose(0, 3, 1, 2, 4)

    ksb, vsb = to_blocks(k_sel), to_blocks(v_sel)
    rb_g = rel_bias.reshape(N_BUCKETS, N_KV, HPG).transpose(1, 0, 2)
    bi = jnp.arange(B)[:, None, None, None]
    gi = jnp.arange(N_KV)[None, :, None, None]

    def local_branches(qc, ic, pc, kwc, vwc, kpc):
        kg = ksb[bi, gi, ic]
        vg = vsb[bi, gi, ic]
        spos = ic[..., None] * SEL_BLOCK + jnp.arange(SEL_BLOCK, dtype=jnp.int32)
        dist_s = pc[None, None, :, None, None] - spos
        bias_s = rb_g[gi[..., None], rel_bucket(dist_s)].transpose(0, 1, 5, 2, 3, 4).astype(jnp.float32)
        logit_s = jnp.einsum('btghd,bgtksd->bghtks', qc, kg).astype(jnp.float32) * scale + bias_s
        shp = logit_s.shape
        flat = shp[:4] + (shp[4] * shp[5],)
        mask_s = (dist_s >= 0).reshape(shp[0], shp[1], 1, shp[3], shp[4] * shp[5])
        p_sel = masked_softmax(logit_s.reshape(flat), mask_s).reshape(shp)
        o_s = jnp.einsum('bghtks,bgtksd->btghd', p_sel.astype(vg.dtype), vg)
        dist_w = pc[:, None] - kpc[None, :]
        mask_w = (dist_w >= 0) & (dist_w < WINDOW) & (kpc >= 0)[None, :]
        logit_w = jnp.einsum('btghd,blgd->bghtl', qc, kwc).astype(jnp.float32) * scale + head_bias(rel_bias, dist_w)
        p_w = masked_softmax(logit_w, mask_w)
        o_w = jnp.einsum('bghtl,blgd->btghd', p_w.astype(vwc.dtype), vwc)
        return o_s, o_w

    if chunked:
        n_chunks = T // QBLOCK
        kwp = jnp.pad(k_win, ((0, 0), (WINDOW, 0), (0, 0), (0, 0)))
        vwp = jnp.pad(v_win, ((0, 0), (WINDOW, 0), (0, 0), (0, 0)))

        def body(c):
            t0 = c * QBLOCK
            qc = lax.dynamic_slice_in_dim(qg, t0, QBLOCK, axis=1)
            ic = lax.dynamic_slice_in_dim(idx, t0, QBLOCK, axis=2)
            pc = q_start + t0 + jnp.arange(QBLOCK, dtype=jnp.int32)
            kwc = lax.dynamic_slice_in_dim(kwp, t0, WINDOW + QBLOCK, axis=1)
            vwc = lax.dynamic_slice_in_dim(vwp, t0, WINDOW + QBLOCK, axis=1)
            kpc = w_start + t0 - WINDOW + jnp.arange(WINDOW + QBLOCK, dtype=jnp.int32)
            return local_branches(qc, ic, pc, kwc, vwc, kpc)

        o_s, o_w = lax.map(body, jnp.arange(n_chunks, dtype=jnp.int32))
        o_s = jnp.moveaxis(o_s, 0, 1).reshape(B, T, N_KV, HPG, HEAD_DIM)
        o_w = jnp.moveaxis(o_w, 0, 1).reshape(B, T, N_KV, HPG, HEAD_DIM)
    else:
        kpc = w_start + jnp.arange(k_win.shape[1], dtype=jnp.int32)
        o_s, o_w = local_branches(qg, idx, qpos, k_win, v_win, kpc)

    g = gates.reshape(B, T, 3, N_KV, HPG)[..., None]
    o = g[:, :, 0] * o_c + g[:, :, 1] * o_s + g[:, :, 2] * o_w
    return o.reshape(B, T, N_HEADS * HEAD_DIM)


def mixer_block(h, lp, rel_bias, past):
    B, T, _ = h.shape
    z = h @ lp['w_in']
    q, kv, ng, a, c, mg = jnp.split(z, SPLITS, axis=-1)
    q = q.reshape(B, T, N_HEADS, HEAD_DIM)
    kv = kv.reshape(B, T, 6, N_KV, HEAD_DIM)
    kv_rows, win_rows = kv[:, :, :4], kv[:, :, 4:]
    if past is None:
        kv_all, win_all = kv_rows, win_rows
        q_start, w_start = 0, 0
        n_win = min(WINDOW, T)
        hist_a = jnp.zeros((B, CONV_A - 1, D_A), h.dtype)
        hist_c = jnp.zeros((B, CONV_C - 1, D_C), h.dtype)
        chunked = True
    else:
        kv_all = jnp.concatenate([past['kv'], kv_rows], axis=1)
        win_all = jnp.concatenate([past['win'], win_rows], axis=1)
        q_start = past['kv'].shape[1]
        n_win = past['win'].shape[1]
        w_start = q_start - n_win
        hist_a, hist_c = past['conv_a'], past['conv_c']
        chunked = False
    o_nsa = nsa_attention(q, kv_all[:, :, 0], kv_all[:, :, 1], kv_all[:, :, 2], kv_all[:, :, 3],
                          win_all[:, :, 0], win_all[:, :, 1],
                          jax.nn.sigmoid(ng.reshape(B, T, 3, N_HEADS)), q_start, w_start, rel_bias,
                          lp['cmp_w1'], lp['cmp_b1'], lp['cmp_w2'], lp['cmp_b2'], lp['cmp_pe'], chunked)
    a_in, a_b, a_c = jnp.split(a, 3, axis=-1)
    y_a, st_a = causal_dwconv(a_c * a_in, hist_a, lp['conv_a_w'])
    c_val, c_gate = jnp.split(c, 2, axis=-1)
    y_c, st_c = causal_dwconv(c_val * jax.nn.sigmoid(c_gate), hist_c, lp['conv_c_w'])
    y_c = jax.nn.silu(layer_norm(y_c + lp['conv_c_b'], lp['conv_c_ln_g'], lp['conv_c_ln_b']))
    gm = jax.nn.sigmoid(mg.reshape(B, T, 3, D_MODEL))
    merged = (gm[:, :, 0] * (o_nsa @ lp['w_nsa_out'])
              + gm[:, :, 1] * ((a_b * y_a) @ lp['w_a_out'])
              + gm[:, :, 2] * (y_c @ lp['w_c_out']))
    return merged @ lp['w_o'], (kv_rows, win_all[:, -n_win:], st_a, st_c)


def cross_attention(h, mem_kv, w_q, w_o):
    B, T, _ = h.shape
    q = (h @ w_q).reshape(B, T, X_HEADS, X_HEAD_DIM)
    logits = jnp.einsum('bthd,bmhd->bhtm', q, mem_kv[:, :, 0]).astype(jnp.float32) * (X_HEAD_DIM ** -0.5)
    p = jax.nn.softmax(logits, axis=-1).astype(mem_kv.dtype)
    o = jnp.einsum('bhtm,bmhd->bthd', p, mem_kv[:, :, 1]).reshape(B, T, X_HEADS * X_HEAD_DIM)
    return o @ w_o


def swiglu(h, w_up, w_down):
    g, u = jnp.split(h @ w_up, 2, axis=-1)
    return (jax.nn.silu(g) * u) @ w_down


def decoder_layer(x, mem_kv, past, lp, rel_bias):
    g = lp['norm_g']
    mix, st = mixer_block(rms_norm(x, g[0]), lp, rel_bias, past)
    x = x + rms_norm(mix, g[1])
    x = x + rms_norm(cross_attention(rms_norm(x, g[2]), mem_kv, lp['w_xq'], lp['w_xo']), g[3])
    x = x + rms_norm(swiglu(rms_norm(x, g[4]), lp['w_ffn_in'], lp['w_ffn_out']), g[5])
    return x, st


def setup_inputs(seed: int = 0) -> dict:
    key = jax.random.key(seed)
    ks = jax.random.split(key, 40)
    f32 = jnp.float32
    n_pages = PAST_LEN // PAGE_SIZE
    n_used = DEC_BATCH * n_pages
    n_pool = n_used + n_used // 4
    win_buf = min(WINDOW, PAST_LEN)

    def nrm(k, shape, scale=1.0):
        return jax.random.normal(k, shape, f32) * scale

    perm = jax.random.permutation(ks[5], n_pool)
    page_table = perm[:n_used].reshape(DEC_BATCH, n_pages).astype(jnp.int32)
    return {
        'x_prompt': nrm(ks[0], (BATCH, SEQ, D_MODEL)),
        'x_sample': nrm(ks[1], (DEC_BATCH, DEC_SEQ, D_MODEL)),
        'mem_prompt': nrm(ks[2], (BATCH, N_MEM, D_MODEL)),
        'cache_kv': nrm(ks[3], (DEPTH, n_pool, PAGE_SIZE, 4, N_KV, HEAD_DIM)),
        'page_table': page_table,
        'cache_win': nrm(ks[4], (DEPTH, DEC_BATCH, win_buf, 2, N_KV, HEAD_DIM)),
        'state_conv_a': nrm(ks[6], (DEPTH, DEC_BATCH, CONV_A - 1, D_A)),
        'state_conv_c': nrm(ks[7], (DEPTH, DEC_BATCH, CONV_C - 1, D_C)),
        'cache_mem': nrm(ks[8], (DEPTH, DEC_BATCH, N_MEM, 2, X_HEADS, X_HEAD_DIM)),
        'rel_bias': nrm(ks[9], (N_BUCKETS, N_HEADS), 0.5),
        'norm_g': 1.0 + nrm(ks[10], (DEPTH, 6, D_MODEL), 0.05),
        'mem_norm_g': 1.0 + nrm(ks[11], (DEPTH, D_MODEL), 0.05),
        'w_in': nrm(ks[12], (DEPTH, D_MODEL, P_IN), D_MODEL ** -0.5),
        'conv_a_w': nrm(ks[13], (DEPTH, CONV_A, D_A), CONV_A ** -0.5),
        'conv_c_w': nrm(ks[14], (DEPTH, CONV_C, D_C), CONV_C ** -0.5),
        'conv_c_b': nrm(ks[15], (DEPTH, D_C), 0.01),
        'conv_c_ln_g': 1.0 + nrm(ks[16], (DEPTH, D_C), 0.05),
        'conv_c_ln_b': nrm(ks[17], (DEPTH, D_C), 0.01),
        'cmp_w1': nrm(ks[18], (DEPTH, 2, CMP_BLOCK * HEAD_DIM, CMP_HIDDEN), (CMP_BLOCK * HEAD_DIM) ** -0.5),
        'cmp_b1': nrm(ks[19], (DEPTH, 2, CMP_HIDDEN), 0.01),
        'cmp_w2': nrm(ks[20], (DEPTH, 2, CMP_HIDDEN, HEAD_DIM), CMP_HIDDEN ** -0.5),
        'cmp_b2': nrm(ks[21], (DEPTH, 2, HEAD_DIM), 0.01),
        'cmp_pe': nrm(ks[22], (DEPTH, 2, CMP_BLOCK, HEAD_DIM), 0.1),
        'w_nsa_out': nrm(ks[23], (DEPTH, N_HEADS * HEAD_DIM, D_MODEL), (N_HEADS * HEAD_DIM) ** -0.5),
        'w_a_out': nrm(ks[24], (DEPTH, D_A, D_MODEL), D_A ** -0.5),
        'w_c_out': nrm(ks[25], (DEPTH, D_C, D_MODEL), D_C ** -0.5),
        'w_o': nrm(ks[26], (DEPTH, D_MODEL, D_MODEL), D_MODEL ** -0.5),
        'w_xq': nrm(ks[27], (DEPTH, D_MODEL, X_HEADS * X_HEAD_DIM), D_MODEL ** -0.5),
        'w_xkv': nrm(ks[28], (DEPTH, D_MODEL, 2 * X_HEADS * X_HEAD_DIM), D_MODEL ** -0.5),
        'w_xo': nrm(ks[29], (DEPTH, X_HEADS * X_HEAD_DIM, D_MODEL), (X_HEADS * X_HEAD_DIM) ** -0.5),
        'w_ffn_in': nrm(ks[30], (DEPTH, D_MODEL, 2 * D_FF), D_MODEL ** -0.5),
        'w_ffn_out': nrm(ks[31], (DEPTH, D_FF, D_MODEL), D_FF ** -0.5),
    }


def reference(x_prompt, x_sample, mem_prompt, cache_kv, page_table, cache_win, state_conv_a, state_conv_c,
              cache_mem, rel_bias, norm_g, mem_norm_g, w_in, conv_a_w, conv_c_w, conv_c_b, conv_c_ln_g,
              conv_c_ln_b, cmp_w1, cmp_b1, cmp_w2, cmp_b2, cmp_pe, w_nsa_out, w_a_out, w_c_out, w_o,
              w_xq, w_xkv, w_xo, w_ffn_in, w_ffn_out):
    n_seq_dec, n_pages = page_table.shape
    n_seq_p = mem_prompt.shape[0]
    yp, ys = x_prompt, x_sample
    kv_p, kv_s, win_p, win_s = [], [], [], []
    ca_p, ca_s, cc_p, cc_s, mem_p = [], [], [], [], []
    for l in range(DEPTH):
        lp = {
            'norm_g': norm_g[l], 'w_in': w_in[l],
            'conv_a_w': conv_a_w[l], 'conv_c_w': conv_c_w[l], 'conv_c_b': conv_c_b[l],
            'conv_c_ln_g': conv_c_ln_g[l], 'conv_c_ln_b': conv_c_ln_b[l],
            'cmp_w1': cmp_w1[l], 'cmp_b1': cmp_b1[l], 'cmp_w2': cmp_w2[l], 'cmp_b2': cmp_b2[l], 'cmp_pe': cmp_pe[l],
            'w_nsa_out': w_nsa_out[l], 'w_a_out': w_a_out[l], 'w_c_out': w_c_out[l], 'w_o': w_o[l],
            'w_xq': w_xq[l], 'w_xo': w_xo[l], 'w_ffn_in': w_ffn_in[l], 'w_ffn_out': w_ffn_out[l],
        }
        mkv_p = (rms_norm(mem_prompt, mem_norm_g[l]) @ w_xkv[l]).reshape(n_seq_p, N_MEM, 2, X_HEADS, X_HEAD_DIM)
        yp, stp = decoder_layer(yp, mkv_p, None, lp, rel_bias)
        past = {
            'kv': cache_kv[l][page_table].reshape(n_seq_dec, n_pages * PAGE_SIZE, 4, N_KV, HEAD_DIM),
            'win': cache_win[l], 'conv_a': state_conv_a[l], 'conv_c': state_conv_c[l],
        }
        ys, sts = decoder_layer(ys, cache_mem[l], past, lp, rel_bias)
        kv_p.append(stp[0]); win_p.append(stp[1]); ca_p.append(stp[2]); cc_p.append(stp[3])
        kv_s.append(sts[0]); win_s.append(sts[1]); ca_s.append(sts[2]); cc_s.append(sts[3])
        mem_p.append(mkv_p)
    return (yp, ys, jnp.stack(kv_p), jnp.stack(kv_s), jnp.stack(win_p), jnp.stack(win_s),
            jnp.stack(ca_p), jnp.stack(ca_s), jnp.stack(cc_p), jnp.stack(cc_s), jnp.stack(mem_p))
```

```python
import functools
import math

import numpy as np
import jax
import jax.numpy as jnp
from jax import lax
from jax.experimental import pallas as pl
from jax.experimental.pallas import tpu as pltpu

F32 = jnp.float32
BF16 = jnp.bfloat16

N_HEADS = 8
N_KV = 2
HPG = N_HEADS // N_KV
HEAD_DIM = 64
CMP_BLOCK = 32
CMP_STRIDE = 16
CMP_HIDDEN = 64
SEL_BLOCK = 64
N_SEL = 16
WINDOW = 512
CONV_A = 3
CONV_C = 31
N_BUCKETS = 32
MAX_DIST = 128
X_HEADS = 4
X_HEAD_DIM = 128
NEG_INF = -1e30
SEL_FORCE = 1e6
SEL_NEG = -1e9
RMS_EPS = 1e-6
LN_EPS = 1e-5

GD = N_KV * HEAD_DIM
Q_W = N_HEADS * HEAD_DIM
TQ = 256
VMEM_LIMIT = 56 * 1024 * 1024


def _bucket_thresholds():
    exact = N_BUCKETS // 2
    thr = {}
    for n in range(0, 4 * MAX_DIST):
        if n < exact:
            b = n
        else:
            b = min(exact + int(math.log(n / exact) / math.log(MAX_DIST / exact) * (N_BUCKETS - exact)),
                    N_BUCKETS - 1)
        for k in range(1, N_BUCKETS):
            if b >= k and k not in thr:
                thr[k] = n
    return [0] + [thr[k] for k in range(1, N_BUCKETS)]


_THR = _bucket_thresholds()


def _cparams(*sem):
    return pltpu.CompilerParams(dimension_semantics=sem, vmem_limit_bytes=VMEM_LIMIT)


def _dot(a, b):
    return jnp.dot(a, b, preferred_element_type=F32)


def _dot_nt(a, b):
    return lax.dot_general(a, b, (((1,), (1,)), ((), ())), preferred_element_type=F32)


def _dot_tn(a, b):
    return lax.dot_general(a, b, (((0,), (0,)), ((), ())), preferred_element_type=F32)


def _rms(x, g):
    return x * lax.rsqrt(jnp.mean(x * x, axis=-1, keepdims=True) + RMS_EPS) * g


def _silu(x):
    return x * jax.nn.sigmoid(x)


def _bias_chain(dist, rb_at):
    val = jnp.zeros(dist.shape, F32) + rb_at(0)
    for k in range(1, N_BUCKETS):
        val = jnp.where(dist >= _THR[k], rb_at(k), val)
    return val


def _full_spec(shape):
    n = len(shape)
    return pl.BlockSpec(shape, lambda *_: (0,) * n)


def _bias_table_kernel(rb_ref, dist_ref, o_ref):
    h = pl.program_id(0)
    d = dist_ref[...]
    val = _bias_chain(d, lambda k: rb_ref[k * N_HEADS + h])
    o_ref[0] = jnp.where(d >= 0, val, NEG_INF)


def _bias_table(rb_flat, dist):
    r, c = dist.shape
    return pl.pallas_call(
        _bias_table_kernel,
        grid=(N_HEADS,),
        in_specs=[pl.BlockSpec(memory_space=pltpu.SMEM), _full_spec((r, c))],
        out_specs=pl.BlockSpec((1, r, c), lambda h: (h, 0, 0)),
        out_shape=jax.ShapeDtypeStruct((N_HEADS, r, c), F32),
        compiler_params=_cparams("arbitrary"),
        name="bias_table",
    )(rb_flat, jnp.asarray(dist, jnp.int32))


def _norm_mm_kernel(x_ref, g_ref, w_ref, o_ref, *maybe_bf):
    xn = _rms(x_ref[...], g_ref[...]).astype(BF16)
    y = _dot(xn, w_ref[...])
    o_ref[...] = y
    if maybe_bf:
        maybe_bf[0][...] = y.astype(BF16)


def _norm_mm(x, g, w, tm, with_bf16=False):
    m, d = x.shape
    n = w.shape[1]
    out_shape = [jax.ShapeDtypeStruct((m, n), F32)]
    out_specs = [pl.BlockSpec((tm, n), lambda i: (i, 0))]
    if with_bf16:
        out_shape.append(jax.ShapeDtypeStruct((m, n), BF16))
        out_specs.append(pl.BlockSpec((tm, n), lambda i: (i, 0)))
    res = pl.pallas_call(
        _norm_mm_kernel,
        grid=(m // tm,),
        in_specs=[pl.BlockSpec((tm, d), lambda i: (i, 0)), _full_spec((1, d)), _full_spec((d, n))],
        out_specs=out_specs,
        out_shape=out_shape,
        compiler_params=_cparams("arbitrary"),
        name="norm_mm",
    )(x, g.reshape(1, d), w)
    return res if with_bf16 else res[0]


def _mm_norm_res_kernel(a_ref, w_ref, g_ref, r_ref, o_ref):
    y = _dot(a_ref[...].astype(BF16), w_ref[...])
    o_ref[...] = r_ref[...] + _rms(y, g_ref[...])


def _mm_norm_res(a, w, g, res, tm):
    m, k = a.shape
    n = w.shape[1]
    return pl.pallas_call(
        _mm_norm_res_kernel,
        grid=(m // tm,),
        in_specs=[pl.BlockSpec((tm, k), lambda i: (i, 0)), _full_spec((k, n)), _full_spec((1, n)),
                  pl.BlockSpec((tm, n), lambda i: (i, 0))],
        out_specs=pl.BlockSpec((tm, n), lambda i: (i, 0)),
        out_shape=jax.ShapeDtypeStruct((m, n), F32),
        compiler_params=_cparams("arbitrary"),
        name="mm_norm_res",
    )(a, w, g.reshape(1, n), res)


_SLOT_POS = (0, 1, 2, 4, 3, 5)
_C_Q = 6 * GD
_C_A = _C_Q + Q_W


def _in_proj_cols(d_model):
    d_a = d_model // 4
    c_c = _C_A + 3 * d_a
    c_mg = c_c + 2 * d_a
    c_ng = c_mg + 3 * d_model
    return c_c, c_mg, c_ng, c_ng + 128


def _in_proj_perm(d_model):
    d_a = d_model // 4
    kv0 = Q_W
    ng0 = kv0 + 6 * GD
    a0 = ng0 + 3 * N_HEADS
    c0 = a0 + 3 * d_a
    mg0 = c0 + 2 * d_a
    end = mg0 + 3 * d_model
    slots_in_order = sorted(range(6), key=lambda s: _SLOT_POS[s])
    idx = []
    for s in slots_in_order:
        idx += list(range(kv0 + s * GD, kv0 + (s + 1) * GD))
    idx += list(range(0, Q_W)) + list(range(a0, c0)) + list(range(c0, mg0)) + list(range(mg0, end))
    idx += list(range(ng0, a0))
    return np.asarray(idx, np.int32)


def _in_proj_kernel(x_ref, g_ref, w_ref, wt_ref, qT_ref, kvT_ref, kc_ref, krm_ref, ngT_ref, a_ref, c_ref, mg_ref,
                    *, cols):
    c_c, c_mg, c_ng, _ = cols
    xn = _rms(x_ref[...], g_ref[...]).astype(BF16)

    def rm(lo, hi):
        return _dot(xn, w_ref[:, lo:hi])

    def fm(lo, hi):
        return _dot_nt(wt_ref[lo:hi, :], xn)

    kc_ref[0] = rm(0, GD)
    kc_ref[1] = rm(GD, 2 * GD)
    krm_ref[...] = rm(2 * GD, 4 * GD).astype(BF16)
    a_ref[...] = rm(_C_A, c_c)
    c_ref[...] = rm(c_c, c_mg)
    mg_ref[...] = jax.nn.sigmoid(rm(c_mg, c_ng))
    qT_ref[...] = (fm(_C_Q, _C_A) * (HEAD_DIM ** -0.5)).astype(BF16)
    for s in range(6):
        kvT_ref[s * GD:(s + 1) * GD, :] = fm(_SLOT_POS[s] * GD, (_SLOT_POS[s] + 1) * GD)
    ngT_ref[...] = jax.nn.sigmoid(fm(c_ng, c_ng + 32))


def _in_proj(x, g, w_all, wt_all, tm):
    b, t, d = x.shape
    cols = _in_proj_cols(d)
    c_c, c_mg, c_ng, n_all = cols
    d_a = d // 4
    row = lambda n: pl.BlockSpec((None, tm, n), lambda i, j: (i, j, 0))
    col = lambda n: pl.BlockSpec((None, n, tm), lambda i, j: (i, 0, j))
    return pl.pallas_call(
        functools.partial(_in_proj_kernel, cols=cols),
        grid=(b, t // tm),
        in_specs=[row(d), _full_spec((1, d)), _full_spec((d, n_all)), _full_spec((n_all, d))],
        out_specs=[col(Q_W), col(6 * GD), pl.BlockSpec((None, 2, tm, GD), lambda i, j: (i, 0, j, 0)), row(2 * GD),
                   col(32), row(3 * d_a), row(2 * d_a), row(3 * d)],
        out_shape=[jax.ShapeDtypeStruct((b, Q_W, t), BF16), jax.ShapeDtypeStruct((b, 6 * GD, t), F32),
                   jax.ShapeDtypeStruct((b, 2, t, GD), F32), jax.ShapeDtypeStruct((b, t, 2 * GD), BF16),
                   jax.ShapeDtypeStruct((b, 32, t), F32), jax.ShapeDtypeStruct((b, t, 3 * d_a), F32),
                   jax.ShapeDtypeStruct((b, t, 2 * d_a), F32), jax.ShapeDtypeStruct((b, t, 3 * d), F32)],
        compiler_params=_cparams("arbitrary", "arbitrary"),
        name="in_proj",
    )(x, g.reshape(1, d), w_all, wt_all)


def _compress_rows(kc_ref, nch, wbd_ref):
    acc = jnp.zeros((nch, 4 * GD), F32)
    for s in range(CMP_STRIDE):
        xs = jnp.concatenate([kc_ref[0, pl.ds(s, nch, stride=CMP_STRIDE), :],
                              kc_ref[1, pl.ds(s, nch, stride=CMP_STRIDE), :]], axis=1).astype(BF16)
        acc = acc + _dot(xs, wbd_ref[s])
    return acc


def _cmp_kernel(kc_ref, qT_ref, ngT_ref, wbd_ref, c0_ref, w2_ref, b2r_ref, w2vT_ref, b2c_ref, biasc_ref, ovT_ref,
                ocT_ref, selT_ref, ck_scr, cvT_scr, *, nch, ns, nsp, n_sel):
    qt = pl.program_id(1)
    tq = qT_ref.shape[1]

    @pl.when(qt == 0)
    def _():
        acc = _compress_rows(kc_ref, nch, wbd_ref)
        h = acc[:, :2 * GD] + pltpu.roll(acc[:, 2 * GD:], nch - 1, 0) + c0_ref[...]
        a = _silu(h).astype(BF16)
        ckv = _dot(a, w2_ref[...]) + b2r_ref[...]
        ck_scr[...] = ckv[:, :GD].astype(BF16)
        cvT_scr[...] = (_dot_nt(w2vT_ref[...], a[:, GD:]) + b2c_ref[...]).astype(BF16)

    lane_t = qt * tq + lax.broadcasted_iota(jnp.int32, (1, tq), 1)
    colvalid = (lane_t >= CMP_BLOCK - 1).astype(F32)
    ck = ck_scr[...]
    jb = lax.broadcasted_iota(jnp.int32, (nsp, tq), 0)
    jq = lane_t // SEL_BLOCK
    for g in range(N_KV):
        psum = jnp.zeros((nch, tq), F32)
        for hh in range(HPG):
            h = g * HPG + hh
            qh = qT_ref[h * HEAD_DIM:(h + 1) * HEAD_DIM, :]
            z = jnp.zeros_like(qh)
            qpad = jnp.concatenate([qh, z] if g == 0 else [z, qh], axis=0)
            s = _dot(ck, qpad) + biasc_ref[h]
            m = jnp.max(s, axis=0, keepdims=True)
            e = jnp.exp(s - m)
            p = e * (colvalid / jnp.sum(e, axis=0, keepdims=True))
            o = _dot(cvT_scr[g * HEAD_DIM:(g + 1) * HEAD_DIM, :], p.astype(BF16))
            ocT_ref[h * HEAD_DIM:(h + 1) * HEAD_DIM, :] = o * ngT_ref[h:h + 1, :]
            psum = psum + p
        hi = psum.astype(BF16)
        r1 = psum - hi.astype(F32)
        mid = r1.astype(BF16)
        lo = (r1 - mid.astype(F32)).astype(BF16)
        ov = ovT_ref[...]
        ps = _dot(ov, hi) + _dot(ov, mid) + _dot(ov, lo)
        future = jb > jq
        forced = (jb == 0) | (jb == jq) | (jb == jq - 1)
        score = jnp.where(future, SEL_NEG, jnp.where(forced, SEL_FORCE, ps))
        if nsp > ns:
            score = jnp.where(jb >= ns, -3e38, score)
        rank = jnp.zeros((nsp, tq), jnp.int32)
        for i in range(ns):
            si = score[i:i + 1, :]
            beats = (si > score) | ((si == score) & (jb > i))
            rank = rank + beats.astype(jnp.int32)
        selT_ref[g] = (rank < n_sel).astype(F32)


def _cmp_select(kc, qT, ngT, cw, biasc, ovT, ns):
    b, _, t, _ = kc.shape
    nch = t // CMP_STRIDE
    nsp = ovT.shape[0]
    wbd, c0, w2, b2r, w2vT, b2c = cw
    return pl.pallas_call(
        functools.partial(_cmp_kernel, nch=nch, ns=ns, nsp=nsp, n_sel=min(N_SEL, ns)),
        grid=(b, t // TQ),
        in_specs=[pl.BlockSpec((None, 2, t, GD), lambda i, j: (i, 0, 0, 0)),
                  pl.BlockSpec((None, Q_W, TQ), lambda i, j: (i, 0, j)),
                  pl.BlockSpec((None, 32, TQ), lambda i, j: (i, 0, j)),
                  _full_spec(wbd.shape), _full_spec(c0.shape), _full_spec(w2.shape), _full_spec(b2r.shape),
                  _full_spec(w2vT.shape), _full_spec(b2c.shape),
                  pl.BlockSpec((N_HEADS, nch, TQ), lambda i, j: (0, 0, j)),
                  _full_spec(ovT.shape)],
        out_specs=[pl.BlockSpec((None, Q_W, TQ), lambda i, j: (i, 0, j)),
                   pl.BlockSpec((None, N_KV, nsp, TQ), lambda i, j: (i, 0, 0, j))],
        out_shape=[jax.ShapeDtypeStruct((b, Q_W, t), F32), jax.ShapeDtypeStruct((b, N_KV, nsp, t), F32)],
        scratch_shapes=[pltpu.VMEM((nch, GD), BF16), pltpu.VMEM((GD, nch), BF16)],
        compiler_params=_cparams("arbitrary", "arbitrary"),
        name="cmp_select",
    )(kc, qT, ngT, wbd, c0, w2, b2r, w2vT, b2c, biasc, ovT)


def _flash_update(carry, s, vT):
    m, l, acc = carry
    m_new = jnp.maximum(m, jnp.max(s, axis=0, keepdims=True))
    alpha = jnp.exp(m - m_new)
    p = jnp.exp(s - m_new)
    l = alpha * l + jnp.sum(p, axis=0, keepdims=True)
    acc = alpha * acc + _dot(vT.astype(BF16), p.astype(BF16))
    return m_new, l, acc


def _selwin_kernel(rb_ref, qT_ref, k_ref, vsel_ref, vwin_ref, selT_ref, ngT_ref, ocT_ref, tb_ref, ex_ref,
                   o_ref, mask_scr):
    qt = pl.program_id(1)
    tq = qT_ref.shape[1]
    tk = tq
    for g in range(N_KV):
        mf = _dot(ex_ref[...], selT_ref[g].astype(BF16))
        mask_scr[g] = (mf - 1.0) * (-NEG_INF)
    rows = lax.broadcasted_iota(jnp.int32, (GD, tq), 0)
    kt1 = jnp.maximum(qt - 1, 0)
    kt2 = jnp.maximum(qt - 2, 0)
    off1 = jnp.where(qt >= 1, 0.0, NEG_INF)
    off2 = jnp.where(qt >= 2, 0.0, NEG_INF)
    init = (jnp.full((1, tq), NEG_INF, F32), jnp.zeros((1, tq), F32), jnp.zeros((HEAD_DIM, tq), F32))

    def head(h, _):
        g = h // HPG
        q64 = qT_ref[pl.ds(pl.multiple_of(h * HEAD_DIM, HEAD_DIM), HEAD_DIM), :]
        q2 = jnp.concatenate([q64, q64], axis=0)
        qpad = jnp.where(rows // HEAD_DIM == g, q2, jnp.zeros_like(q2))
        c31 = rb_ref[(N_BUCKETS - 1) * N_HEADS + h]
        vrow = pl.multiple_of(g * HEAD_DIM, HEAD_DIM)

        def tile(kt, lane0):
            ks = pl.multiple_of(kt * tk, tk)
            k2 = k_ref[pl.ds(ks, tk), lane0:lane0 + GD]
            return ks, _dot(k2, qpad)

        ks, s = tile(qt, 0)
        c = _flash_update(init, s + tb_ref[h, 0] + mask_scr[g, pl.ds(ks, tk), :],
                          vsel_ref[pl.ds(vrow, HEAD_DIM), pl.ds(ks, tk)])
        ks, s = tile(kt1, 0)
        c = _flash_update(c, s + tb_ref[h, 1] + mask_scr[g, pl.ds(ks, tk), :] + off1,
                          vsel_ref[pl.ds(vrow, HEAD_DIM), pl.ds(ks, tk)])

        def far(kt, c):
            ks, s = tile(kt, 0)
            return _flash_update(c, s + c31 + mask_scr[g, pl.ds(ks, tk), :],
                                 vsel_ref[pl.ds(vrow, HEAD_DIM), pl.ds(ks, tk)])

        m, l, acc = lax.fori_loop(0, kt1, far, c)
        o_s = acc / l
        ks, s = tile(qt, GD)
        c = _flash_update(init, s + tb_ref[h, 0], vwin_ref[pl.ds(vrow, HEAD_DIM), pl.ds(ks, tk)])
        ks, s = tile(kt1, GD)
        c = _flash_update(c, s + tb_ref[h, 1] + off1, vwin_ref[pl.ds(vrow, HEAD_DIM), pl.ds(ks, tk)])
        ks, s = tile(kt2, GD)
        m, l, acc = _flash_update(c, s + tb_ref[h, 2] + off2, vwin_ref[pl.ds(vrow, HEAD_DIM), pl.ds(ks, tk)])
        o_w = acc / l
        hrow = pl.ds(pl.multiple_of(h * HEAD_DIM, HEAD_DIM), HEAD_DIM)
        out = (ocT_ref[hrow, :] + ngT_ref[pl.ds(N_HEADS + h, 1), :] * o_s
               + ngT_ref[pl.ds(2 * N_HEADS + h, 1), :] * o_w)
        o_ref[hrow, :] = out.astype(o_ref.dtype)
        return 0

    lax.fori_loop(0, N_HEADS, head, 0)


def _selwin(rb_flat, qT, krm, kvT, selT, ngT, ocT, tb, ex):
    b, _, t = qT.shape
    nsp = selT.shape[2]
    return pl.pallas_call(
        _selwin_kernel,
        grid=(b, t // TQ),
        in_specs=[pl.BlockSpec(memory_space=pltpu.SMEM),
                  pl.BlockSpec((None, Q_W, TQ), lambda i, j: (i, 0, j)),
                  pl.BlockSpec((None, t, 2 * GD), lambda i, j: (i, 0, 0)),
                  pl.BlockSpec((None, GD, t), lambda i, j: (i, 3, 0)),
                  pl.BlockSpec((None, GD, t), lambda i, j: (i, 5, 0)),
                  pl.BlockSpec((None, N_KV, nsp, TQ), lambda i, j: (i, 0, 0, j)),
                  pl.BlockSpec((None, 32, TQ), lambda i, j: (i, 0, j)),
                  pl.BlockSpec((None, Q_W, TQ), lambda i, j: (i, 0, j)),
                  _full_spec(tb.shape), _full_spec(ex.shape)],
        out_specs=pl.BlockSpec((None, Q_W, TQ), lambda i, j: (i, 0, j)),
        out_shape=jax.ShapeDtypeStruct((b, Q_W, t), BF16),
        scratch_shapes=[pltpu.VMEM((N_KV, t, TQ), F32)],
        compiler_params=_cparams("arbitrary", "arbitrary"),
        name="selwin",
    )(rb_flat, qT, krm, kvT, kvT, selT, ngT, ocT, tb, ex)


def _layer_norm(x, g, b):
    mu = jnp.mean(x, axis=-1, keepdims=True)
    xc = x - mu
    var = jnp.mean(xc * xc, axis=-1, keepdims=True)
    return xc * lax.rsqrt(var + LN_EPS) * g + b


def _mix_kernel(onT_ref, a_ref, c_ref, mg_ref, x_ref, wn_ref, wa_ref, wc_ref, wo_ref, caw_ref, ccw_ref, ccb_ref,
                lng_ref, lnb_ref, g_ref, y_ref, sta_ref, stc_ref, ua, vc):
    ti = pl.program_id(1)
    tm = x_ref.shape[0]
    d_a = wa_ref.shape[0]
    d = x_ref.shape[1]
    ha, hc = 8, 32

    @pl.when(ti == 0)
    def _():
        ua[0:ha, :] = jnp.zeros((ha, d_a), F32)
        vc[0:hc, :] = jnp.zeros((hc, d_a), F32)

    a = a_ref[...]
    a_b = a[:, d_a:2 * d_a]
    u = a[:, 2 * d_a:] * a[:, :d_a]
    ua[ha:ha + tm, :] = u
    y_a = jnp.zeros((tm, d_a), F32)
    for j in range(CONV_A):
        y_a = y_a + caw_ref[j:j + 1, :] * ua[pl.ds(ha - (CONV_A - 1) + j, tm), :]
    c = c_ref[...]
    v = c[:, :d_a] * jax.nn.sigmoid(c[:, d_a:])
    vc[hc:hc + tm, :] = v
    y_c = jnp.zeros((tm, d_a), F32)
    for j in range(CONV_C):
        y_c = y_c + ccw_ref[j:j + 1, :] * vc[pl.ds(hc - (CONV_C - 1) + j, tm), :]
    y_c = _silu(_layer_norm(y_c + ccb_ref[...], lng_ref[...], lnb_ref[...]))
    ta = ua[tm:tm + ha, :]
    tc = vc[tm:tm + hc, :]
    sta_ref[...] = ta
    stc_ref[...] = tc
    ua[0:ha, :] = ta
    vc[0:hc, :] = tc

    gm = mg_ref[...]
    merged = (gm[:, :d] * _dot_tn(onT_ref[...], wn_ref[...])
              + gm[:, d:2 * d] * _dot((a_b * y_a).astype(BF16), wa_ref[...])
              + gm[:, 2 * d:] * _dot(y_c.astype(BF16), wc_ref[...]))
    mix = _dot(merged.astype(BF16), wo_ref[...])
    y_ref[...] = x_ref[...] + _rms(mix, g_ref[...])


def _mix(onT, a, c, mg, x, wn, wa, wc, wo, caw, ccw, ccb, lng, lnb, g, tm):
    b, t, d = x.shape
    d_a = wa.shape[0]
    row = lambda n: pl.BlockSpec((None, tm, n), lambda i, j: (i, j, 0))
    vec = lambda v: v.reshape(1, -1)
    return pl.pallas_call(
        _mix_kernel,
        grid=(b, t // tm),
        in_specs=[pl.BlockSpec((None, Q_W, tm), lambda i, j: (i, 0, j)), row(3 * d_a), row(2 * d_a), row(3 * d), row(d),
                  _full_spec(wn.shape), _full_spec(wa.shape), _full_spec(wc.shape), _full_spec(wo.shape),
                  _full_spec(caw.shape), _full_spec(ccw.shape), _full_spec((1, d_a)), _full_spec((1, d_a)),
                  _full_spec((1, d_a)), _full_spec((1, d))],
        out_specs=[row(d), pl.BlockSpec((None, 8, d_a), lambda i, j: (i, 0, 0)),
                   pl.BlockSpec((None, 32, d_a), lambda i, j: (i, 0, 0))],
        out_shape=[jax.ShapeDtypeStruct((b, t, d), F32), jax.ShapeDtypeStruct((b, 8, d_a), F32),
                   jax.ShapeDtypeStruct((b, 32, d_a), F32)],
        scratch_shapes=[pltpu.VMEM((8 + tm, d_a), F32), pltpu.VMEM((32 + tm, d_a), F32)],
        compiler_params=_cparams("arbitrary", "arbitrary"),
        name="mix",
    )(onT, a, c, mg, x, wn, wa, wc, wo, caw, ccw, vec(ccb), vec(lng), vec(lnb), vec(g))


def _xattn_kernel(x_ref, g2_ref, wq_ref, mkv_ref, wxo_ref, g3_ref, y_ref):
    x = x_ref[...]
    hw = X_HEADS * X_HEAD_DIM
    q = _dot(_rms(x, g2_ref[...]).astype(BF16), wq_ref[...]).astype(BF16)
    outs = []
    for h in range(X_HEADS):
        lo = h * X_HEAD_DIM
        s = _dot_nt(q[:, lo:lo + X_HEAD_DIM], mkv_ref[:, lo:lo + X_HEAD_DIM]) * (X_HEAD_DIM ** -0.5)
        m = jnp.max(s, axis=-1, keepdims=True)
        e = jnp.exp(s - m)
        p = e / jnp.sum(e, axis=-1, keepdims=True)
        outs.append(_dot(p.astype(BF16), mkv_ref[:, hw + lo:hw + lo + X_HEAD_DIM]))
    o = jnp.concatenate(outs, axis=-1).astype(BF16)
    y_ref[...] = x + _rms(_dot(o, wxo_ref[...]), g3_ref[...])


def _xattn(x, g2, wq, mkv_bf, wxo, g3, tm):
    b, t, d = x.shape
    n_mem, w2 = mkv_bf.shape[1:]
    return pl.pallas_call(
        _xattn_kernel,
        grid=(b, t // tm),
        in_specs=[pl.BlockSpec((None, tm, d), lambda i, j: (i, j, 0)), _full_spec((1, d)), _full_spec(wq.shape),
                  pl.BlockSpec((None, n_mem, w2), lambda i, j: (i, 0, 0)), _full_spec(wxo.shape), _full_spec((1, d))],
        out_specs=pl.BlockSpec((None, tm, d), lambda i, j: (i, j, 0)),
        out_shape=jax.ShapeDtypeStruct((b, t, d), F32),
        compiler_params=_cparams("arbitrary", "arbitrary"),
        name="xattn",
    )(x, g2.reshape(1, d), wq, mkv_bf, wxo, g3.reshape(1, d))


def _ffn_kernel(x_ref, g4_ref, win_ref, wout_ref, g5_ref, y_ref, *, d_ff, chunk):
    x = x_ref[...]
    xn = _rms(x, g4_ref[...]).astype(BF16)
    acc = jnp.zeros(x.shape, F32)
    for c in range(d_ff // chunk):
        lo = c * chunk
        gate = _dot(xn, win_ref[:, lo:lo + chunk])
        up = _dot(xn, win_ref[:, d_ff + lo:d_ff + lo + chunk])
        acc = acc + _dot((_silu(gate) * up).astype(BF16), wout_ref[lo:lo + chunk, :])
    y_ref[...] = x + _rms(acc, g5_ref[...])


def _ffn(x, g4, win, wout, g5, tm):
    m, d = x.shape
    d_ff = wout.shape[0]
    return pl.pallas_call(
        functools.partial(_ffn_kernel, d_ff=d_ff, chunk=256),
        grid=(m // tm,),
        in_specs=[pl.BlockSpec((tm, d), lambda i: (i, 0)), _full_spec((1, d)), _full_spec(win.shape),
                  _full_spec(wout.shape), _full_spec((1, d))],
        out_specs=pl.BlockSpec((tm, d), lambda i: (i, 0)),
        out_shape=jax.ShapeDtypeStruct((m, d), F32),
        compiler_params=_cparams("arbitrary"),
        name="ffn",
    )(x, g4.reshape(1, d), win, wout, g5.reshape(1, d))


def _row_to_col(row):
    n = row.shape[1]
    eye = lax.broadcasted_iota(jnp.int32, (n, n), 0) == lax.broadcasted_iota(jnp.int32, (n, n), 1)
    return jnp.sum(jnp.where(eye, jnp.broadcast_to(row, (n, n)), 0.0), axis=1, keepdims=True)


def _paged_compress_kernel(pt_ref, *refs, pps):
    page_refs = refs[:pps]
    wbd_ref, o_ref, xs = refs[pps:]
    psz = page_refs[0].shape[1]
    for k in range(pps):
        pt = page_refs[k][...].T
        xs[0, k * psz:(k + 1) * psz, :] = pt[:, :GD]
        xs[1, k * psz:(k + 1) * psz, :] = pt[:, GD:]
    o_ref[...] = _compress_rows(xs, pps * psz // CMP_STRIDE, wbd_ref)


def _paged_compress(cache_t, layer, page_table, wbd, pps):
    nseq, n_pages = page_table.shape
    psz = cache_t.shape[3]
    cpp = psz // CMP_STRIDE
    page_spec = lambda k: pl.BlockSpec((None, None, 2 * GD, psz),
                                       lambda b, pg, pt: (layer, pt[b, pg * pps + k], 0, 0))
    return pl.pallas_call(
        functools.partial(_paged_compress_kernel, pps=pps),
        grid_spec=pltpu.PrefetchScalarGridSpec(
            num_scalar_prefetch=1,
            grid=(nseq, n_pages // pps),
            in_specs=[page_spec(k) for k in range(pps)] + [pl.BlockSpec(wbd.shape, lambda b, pg, pt: (0, 0, 0))],
            out_specs=pl.BlockSpec((None, pps * cpp, 4 * GD), lambda b, pg, pt: (b, pg, 0)),
            scratch_shapes=[pltpu.VMEM((2, pps * psz, GD), F32)],
        ),
        out_shape=jax.ShapeDtypeStruct((nseq, n_pages * cpp, 4 * GD), F32),
        compiler_params=_cparams("arbitrary", "arbitrary"),
        name="paged_compress",
    )(page_table, *([cache_t] * pps), wbd)


def _dec_cmp_kernel(h12_ref, kcn_ref, q_ref, ng_ref, wbd0_ref, c0_ref, w2_ref, b2r_ref, bias_ref, ov_ref,
                    oc_ref, idx_ref, *, ns, jq, n_sel):
    nb = h12_ref.shape[0]
    nsp = ov_ref.shape[1]
    acc = h12_ref[...]
    newc = _dot(kcn_ref[...].astype(BF16), wbd0_ref[:, 2 * GD:])
    h2 = pltpu.roll(acc[:, 2 * GD:], nb - 1, 0)
    h2 = jnp.where(lax.broadcasted_iota(jnp.int32, (nb, 2 * GD), 0) == nb - 1, newc, h2)
    a = _silu(acc[:, :2 * GD] + h2 + c0_ref[...]).astype(BF16)
    ckv = _dot(a, w2_ref[...]) + b2r_ref[...]
    q = q_ref[...] * (HEAD_DIM ** -0.5)
    grp = lax.broadcasted_iota(jnp.int32, (N_HEADS, GD), 0) // HPG
    q2 = jnp.where(lax.broadcasted_iota(jnp.int32, (N_HEADS, GD), 1) // HEAD_DIM == grp,
                   jnp.concatenate([q, q], axis=1), 0.0)
    s = _dot_nt(q2.astype(BF16), ckv[:, :GD].astype(BF16)) + bias_ref[...]
    m = jnp.max(s, axis=-1, keepdims=True)
    e = jnp.exp(s - m)
    p = e / jnp.sum(e, axis=-1, keepdims=True)
    o2 = _dot(p.astype(BF16), ckv[:, GD:].astype(BF16))
    hg = lax.broadcasted_iota(jnp.int32, (N_HEADS, HEAD_DIM), 0) // HPG
    o = jnp.where(hg == 0, o2[:, :HEAD_DIM], o2[:, HEAD_DIM:])
    oc_ref[...] = o * jax.nn.sigmoid(ng_ref[:, 0:1])
    psum = jnp.concatenate([jnp.sum(p[g * HPG:(g + 1) * HPG], axis=0, keepdims=True) for g in range(N_KV)], axis=0)
    hi = psum.astype(BF16)
    r1 = psum - hi.astype(F32)
    mid = r1.astype(BF16)
    lo = (r1 - mid.astype(F32)).astype(BF16)
    ov = ov_ref[...]
    ps = _dot(hi, ov) + _dot(mid, ov) + _dot(lo, ov)
    jb = lax.broadcasted_iota(jnp.int32, (N_KV, nsp), 1)
    forced = (jb == 0) | (jb == jq) | (jb == jq - 1)
    score = jnp.where(jb > jq, SEL_NEG, jnp.where(forced, SEL_FORCE, ps))
    score = jnp.where(jb >= ns, -3e38, score)
    kcol = lax.broadcasted_iota(jnp.int32, (N_KV, N_SEL), 1)
    picked = jnp.zeros((N_KV, N_SEL), jnp.int32)
    for k in range(n_sel):
        mx = jnp.max(score, axis=-1, keepdims=True)
        first = jnp.min(jnp.where(score == mx, jb, nsp), axis=-1, keepdims=True)
        picked = jnp.where(kcol == k, first, picked)
        score = jnp.where(jb == first, -3.4e38, score)
    idx_ref[...] = picked


def _dec_cmp(h12, kcn, q3, ngc, cw, bias, ov, past):
    nseq, nb, _ = h12.shape
    wbd, c0, w2, b2r, _, _ = cw
    ns = -(-(past + 1) // SEL_BLOCK)
    per_seq = lambda *dims: pl.BlockSpec((None,) + dims, lambda b: (b,) + (0,) * len(dims))
    return pl.pallas_call(
        functools.partial(_dec_cmp_kernel, ns=ns, jq=past // SEL_BLOCK, n_sel=min(N_SEL, ns)),
        grid=(nseq,),
        in_specs=[per_seq(nb, 4 * GD), per_seq(1, 2 * GD), per_seq(N_HEADS, HEAD_DIM), per_seq(N_HEADS, 3),
                  _full_spec((2 * GD, 4 * GD)), _full_spec(c0.shape), _full_spec(w2.shape), _full_spec(b2r.shape),
                  _full_spec(bias.shape), _full_spec(ov.shape)],
        out_specs=[per_seq(N_HEADS, HEAD_DIM), per_seq(N_KV, N_SEL)],
        out_shape=[jax.ShapeDtypeStruct((nseq, N_HEADS, HEAD_DIM), F32),
                   jax.ShapeDtypeStruct((nseq, N_KV, N_SEL), jnp.int32)],
        compiler_params=_cparams("arbitrary"),
        name="dec_cmp",
    )(h12, kcn, q3, ngc, wbd[0], c0, w2, b2r, bias, ov)


def _dec_sel_kernel(pt_ref, sel_ref, *refs, past, n_sel):
    k_refs = refs[:n_sel]
    v_refs = refs[n_sel:2 * n_sel]
    q_ref, kn_ref, vn_ref, ng_ref, rbT_ref, o_ref = refs[2 * n_sel:]
    b = pl.program_id(0)
    g = pl.program_id(1)
    psz = k_refs[0].shape[1]
    bpp = psz // SEL_BLOCK
    n_cached = past // SEL_BLOCK
    q = q_ref[...] * (HEAD_DIM ** -0.5)
    qb = q.astype(BF16)
    rb_at = lambda k: rbT_ref[:, k:k + 1]
    lane = lax.broadcasted_iota(jnp.int32, (1, psz), 1)
    ss = []
    for k in range(n_sel):
        j = sel_ref[b * N_KV + g, k]
        jc = jnp.minimum(j, n_cached - 1)
        pos = (jc // bpp) * psz + lane
        valid = (pos // SEL_BLOCK == j) & (j < n_cached)
        s = _dot(qb, k_refs[k][...].astype(BF16)) + _bias_chain(past - pos, rb_at)
        ss.append(jnp.where(valid, s, NEG_INF))
    s_new = jnp.sum(q * kn_ref[...], axis=-1, keepdims=True) + rb_at(0)
    m = s_new
    for s in ss:
        m = jnp.maximum(m, jnp.max(s, axis=-1, keepdims=True))
    e_new = jnp.exp(s_new - m)
    l = e_new
    acc = e_new * vn_ref[...]
    for k in range(n_sel):
        e = jnp.exp(ss[k] - m)
        l = l + jnp.sum(e, axis=-1, keepdims=True)
        acc = acc + _dot_nt(e.astype(BF16), v_refs[k][...].astype(BF16))
    o_ref[...] = (acc / l) * jax.nn.sigmoid(ng_ref[:, 1:2])


def _dec_sel(cache_t, layer, page_table, sel, q4, kn, vn, ngc4, rbT, past):
    nseq = page_table.shape[0]
    psz = cache_t.shape[3]
    bpp = psz // SEL_BLOCK
    n_cached = past // SEL_BLOCK
    n_sel = sel.shape[-1]
    sel2 = sel.reshape(nseq * N_KV, n_sel)

    def page_spec(k, slot):
        def imap(b, g, pt, sl):
            j = jnp.minimum(sl[b * N_KV + g, k], n_cached - 1)
            return (layer, pt[b, j // bpp], slot * N_KV + g, 0)
        return pl.BlockSpec((None, None, HEAD_DIM, psz), imap)

    per = lambda *dims: pl.BlockSpec((None, None) + dims, lambda b, g, pt, sl: (b, g) + (0,) * len(dims))
    return pl.pallas_call(
        functools.partial(_dec_sel_kernel, past=past, n_sel=n_sel),
        grid_spec=pltpu.PrefetchScalarGridSpec(
            num_scalar_prefetch=2,
            grid=(nseq, N_KV),
            in_specs=[page_spec(k, 2) for k in range(n_sel)] + [page_spec(k, 3) for k in range(n_sel)]
            + [per(HPG, HEAD_DIM), per(1, HEAD_DIM), per(1, HEAD_DIM), per(HPG, 3),
               pl.BlockSpec((None, HPG, N_BUCKETS), lambda b, g, pt, sl: (g, 0, 0))],
            out_specs=per(HPG, HEAD_DIM),
        ),
        out_shape=jax.ShapeDtypeStruct((nseq, N_KV, HPG, HEAD_DIM), F32),
        compiler_params=_cparams("arbitrary", "arbitrary"),
        name="dec_sel",
    )(page_table, sel2, *([cache_t] * (2 * n_sel)), q4, kn, vn, ngc4, rbT)


def _dec_win_kernel(win_ref, q_ref, kn_ref, vn_ref, ng_ref, rbT_ref, o_ref, nw_ref):
    n_win = win_ref.shape[3]
    lane = lax.broadcasted_iota(jnp.int32, (1, n_win), 1)
    dist = n_win - lane
    lane_f = lax.broadcasted_iota(jnp.int32, (HEAD_DIM, n_win), 1)
    for g in range(N_KV):
        rb_at = lambda k: rbT_ref[g, :, k:k + 1]
        q = q_ref[g] * (HEAD_DIM ** -0.5)
        kT = win_ref[0, g]
        vT = win_ref[1, g]
        s = _dot(q.astype(BF16), kT.astype(BF16)) + _bias_chain(dist, rb_at)
        s = jnp.where(dist < WINDOW, s, NEG_INF)
        s_new = jnp.sum(q * kn_ref[g], axis=-1, keepdims=True) + rb_at(0)
        m = jnp.maximum(s_new, jnp.max(s, axis=-1, keepdims=True))
        e = jnp.exp(s - m)
        e_new = jnp.exp(s_new - m)
        l = e_new + jnp.sum(e, axis=-1, keepdims=True)
        acc = e_new * vn_ref[g] + _dot_nt(e.astype(BF16), vT.astype(BF16))
        o_ref[g] = (acc / l) * jax.nn.sigmoid(ng_ref[g, :, 2:3])
        nw_ref[0, g] = jnp.where(lane_f == n_win - 1, _row_to_col(kn_ref[g]), pltpu.roll(kT, n_win - 1, 1))
        nw_ref[1, g] = jnp.where(lane_f == n_win - 1, _row_to_col(vn_ref[g]), pltpu.roll(vT, n_win - 1, 1))


def _dec_win(win_t, layer, q4, kn, vn, ngc4, rbT):
    nseq, n_win = win_t.shape[1], win_t.shape[5]
    per = lambda *dims: pl.BlockSpec((None,) + dims, lambda b: (b,) + (0,) * len(dims))
    return pl.pallas_call(
        _dec_win_kernel,
        grid=(nseq,),
        in_specs=[pl.BlockSpec((None, None, 2, N_KV, HEAD_DIM, n_win), lambda b: (layer, b, 0, 0, 0, 0)),
                  per(N_KV, HPG, HEAD_DIM), per(N_KV, 1, HEAD_DIM), per(N_KV, 1, HEAD_DIM), per(N_KV, HPG, 3),
                  _full_spec(rbT.shape)],
        out_specs=[per(N_KV, HPG, HEAD_DIM), per(2, N_KV, HEAD_DIM, n_win)],
        out_shape=[jax.ShapeDtypeStruct((nseq, N_KV, HPG, HEAD_DIM), F32),
                   jax.ShapeDtypeStruct((nseq, 2, N_KV, HEAD_DIM, n_win), F32)],
        compiler_params=_cparams("arbitrary"),
        name="dec_win",
    )(win_t, q4, kn, vn, ngc4, rbT)


def _dec_mix_kernel(z_ref, on_ref, x_ref, sa_ref, sc_ref, wn_ref, wa_ref, wc_ref, wo_ref, caw_ref, ccw_ref, ccb_ref,
                    lng_ref, lnb_ref, g_ref, y_ref, nsa_ref, nsc_ref, *, cols):
    c_c, c_mg, _, _ = cols
    d_a = wa_ref.shape[0]
    d = x_ref.shape[1]
    a_in = z_ref[:, _C_A:_C_A + d_a]
    a_b = z_ref[:, _C_A + d_a:_C_A + 2 * d_a]
    u = z_ref[:, _C_A + 2 * d_a:c_c] * a_in
    h0 = sa_ref[:, :d_a]
    h1 = sa_ref[:, d_a:]
    y_a = caw_ref[0:1, :] * h0 + caw_ref[1:2, :] * h1 + caw_ref[2:3, :] * u
    nsa_ref[:, :d_a] = h1
    nsa_ref[:, d_a:] = u
    v = z_ref[:, c_c:c_c + d_a] * jax.nn.sigmoid(z_ref[:, c_c + d_a:c_mg])
    y_c = ccw_ref[CONV_C - 1:CONV_C, :] * v
    for j in range(CONV_C - 1):
        y_c = y_c + ccw_ref[j:j + 1, :] * sc_ref[j]
        nsc_ref[j] = sc_ref[j + 1] if j + 1 < CONV_C - 1 else v
    y_c = _silu(_layer_norm(y_c + ccb_ref[...], lng_ref[...], lnb_ref[...]))
    gm = jax.nn.sigmoid(z_ref[:, c_mg:c_mg + 3 * d])
    merged = (gm[:, :d] * _dot(on_ref[...].astype(BF16), wn_ref[...])
              + gm[:, d:2 * d] * _dot((a_b * y_a).astype(BF16), wa_ref[...])
              + gm[:, 2 * d:] * _dot(y_c.astype(BF16), wc_ref[...]))
    y_ref[...] = x_ref[...] + _rms(_dot(merged.astype(BF16), wo_ref[...]), g_ref[...])


def _dec_mix(z, on, x, sa, sc_t, layer, lw, g):
    nseq, d = x.shape
    d_a = lw['w_a_out'].shape[0]
    n_hist = sc_t.shape[1]
    vec = lambda v: v.reshape(1, -1)
    args = (z, on, x, sa, sc_t, lw['w_nsa_out'], lw['w_a_out'], lw['w_c_out'], lw['w_o'], lw['conv_a_w'],
            lw['conv_c_w'], vec(lw['conv_c_b']), vec(lw['conv_c_ln_g']), vec(lw['conv_c_ln_b']), vec(g))
    in_specs = [_full_spec(a.shape) for a in args]
    in_specs[4] = pl.BlockSpec((None, n_hist, nseq, d_a), lambda i: (layer, 0, 0, 0))
    return pl.pallas_call(
        functools.partial(_dec_mix_kernel, cols=_in_proj_cols(d)),
        grid=(1,),
        in_specs=in_specs,
        out_specs=[_full_spec((nseq, d)), _full_spec((nseq, 2 * d_a)), _full_spec((n_hist, nseq, d_a))],
        out_shape=[jax.ShapeDtypeStruct((nseq, d), F32), jax.ShapeDtypeStruct((nseq, 2 * d_a), F32),
                   jax.ShapeDtypeStruct((n_hist, nseq, d_a), F32)],
        compiler_params=_cparams("arbitrary"),
        name="dec_mix",
    )(*args)


def _dec_xattn_kernel(q_ref, mem_ref, o_ref):
    hw = X_HEADS * X_HEAD_DIM
    q = q_ref[...]
    row = lax.broadcasted_iota(jnp.int32, (X_HEADS, hw), 0)
    own = lax.broadcasted_iota(jnp.int32, (X_HEADS, hw), 1) // X_HEAD_DIM == row
    q2 = jnp.where(own, jnp.concatenate([q] * X_HEADS, axis=1), 0.0)
    s = _dot_nt(q2.astype(BF16), mem_ref[:, :hw].astype(BF16)) * (X_HEAD_DIM ** -0.5)
    m = jnp.max(s, axis=-1, keepdims=True)
    e = jnp.exp(s - m)
    p = e / jnp.sum(e, axis=-1, keepdims=True)
    r = _dot(p.astype(BF16), mem_ref[:, hw:].astype(BF16))
    o_ref[...] = jnp.sum(jnp.where(own, r, 0.0), axis=0, keepdims=True)


def _dec_xattn(q3, mem, layer):
    nseq = q3.shape[0]
    n_mem, w2 = mem.shape[2:]
    return pl.pallas_call(
        _dec_xattn_kernel,
        grid=(nseq,),
        in_specs=[pl.BlockSpec((None, X_HEADS, X_HEAD_DIM), lambda b: (b, 0, 0)),
                  pl.BlockSpec((None, None, n_mem, w2), lambda b: (layer, b, 0, 0))],
        out_specs=pl.BlockSpec((None, 1, w2 // 2), lambda b: (b, 0, 0)),
        out_shape=jax.ShapeDtypeStruct((nseq, 1, w2 // 2), F32),
        compiler_params=_cparams("arbitrary"),
        name="dec_xattn",
    )(q3, mem)


def _sample_layer(x, layer, lw, dec):
    nseq, d = x.shape
    past = dec['past']
    z = _norm_mm(x, lw['norm_g'][0], lw['w_all'], nseq)
    sl = lambda pos: z[:, pos * GD:(pos + 1) * GD]
    q = z[:, _C_Q:_C_A]
    c_ng = _in_proj_cols(d)[2]
    ngc = z[:, c_ng:c_ng + 3 * N_HEADS].reshape(nseq, 3, N_HEADS).transpose(0, 2, 1)
    ngc4 = ngc.reshape(nseq, N_KV, HPG, 3)
    q4 = q.reshape(nseq, N_KV, HPG, HEAD_DIM)
    row4 = lambda a: a.reshape(nseq, N_KV, 1, HEAD_DIM)
    h12 = _paged_compress(dec['cache_t'], layer, dec['page_table'], lw['cmp'][0], dec['pps'])
    oc, sel = _dec_cmp(h12, z[:, :2 * GD].reshape(nseq, 1, 2 * GD), q.reshape(nseq, N_HEADS, HEAD_DIM), ngc,
                       lw['cmp'], dec['bias_c'], dec['ov'], past)
    o_s = _dec_sel(dec['cache_t'], layer, dec['page_table'], sel, q4, row4(sl(2)), row4(sl(4)), ngc4,
                   dec['rbT'], past)
    o_w, new_win = _dec_win(dec['win_t'], layer, q4, row4(sl(3)), row4(sl(5)), ngc4, dec['rbT'])
    on = (oc.reshape(nseq, Q_W) + o_s.reshape(nseq, Q_W) + o_w.reshape(nseq, Q_W))
    y1, new_sa, new_sc = _dec_mix(z, on, x, dec['state_a'][layer].reshape(nseq, -1), dec['state_c_t'], layer, lw,
                                  lw['norm_g'][1])
    xq = _norm_mm(y1, lw['norm_g'][2], lw['w_xq'], nseq)
    ox = _dec_xattn(xq.reshape(nseq, X_HEADS, X_HEAD_DIM), dec['mem'], layer)
    y2 = _mm_norm_res(ox.reshape(nseq, -1), lw['w_xo'], lw['norm_g'][3], y1, nseq)
    y3 = _ffn(y2, lw['norm_g'][4], lw['w_ffn_in'], lw['w_ffn_out'], lw['norm_g'][5], nseq)
    kv_rows = jnp.concatenate([sl(0), sl(1), sl(2), sl(4)], axis=1).reshape(nseq, 1, 4, N_KV, HEAD_DIM)
    win_rows = new_win.transpose(0, 4, 1, 2, 3)
    st = (kv_rows, win_rows, new_sa.reshape(nseq, CONV_A - 1, -1), new_sc.transpose(1, 0, 2))
    return y3, st


def _prep_compress(w1, b1, w2, b2, pe):
    half = CMP_STRIDE * HEAD_DIM
    eye_g = jnp.eye(N_KV, dtype=F32)
    eye_s = jnp.eye(2, dtype=F32)
    w1r = w1.reshape(2, 2, CMP_STRIDE, HEAD_DIM, CMP_HIDDEN)
    wbd = jnp.einsum('khsde,kl,gj->skgdhlje', w1r, eye_s, eye_g)
    wbd = wbd.reshape(CMP_STRIDE, 2 * GD, 4 * GD).astype(BF16)
    c0 = jnp.einsum('kf,kfe->ke', pe.reshape(2, -1), w1) + b1
    c0 = jnp.broadcast_to(c0[:, None, :], (2, N_KV, CMP_HIDDEN)).reshape(1, 2 * GD)
    w2bd = jnp.einsum('ked,kl,gj->kgeljd', w2, eye_s, eye_g).reshape(2 * GD, 2 * GD).astype(BF16)
    b2r = jnp.broadcast_to(b2[:, None, :], (2, N_KV, HEAD_DIM)).reshape(1, 2 * GD)
    w2vT = jnp.einsum('ed,gj->gdje', w2[1], eye_g).reshape(GD, GD).astype(BF16)
    b2c = jnp.broadcast_to(b2[1][None, :], (N_KV, HEAD_DIM)).reshape(GD, 1)
    return wbd, c0, w2bd, b2r, w2vT, b2c


def _overlap_T(nb_pad, ns, ns_pad):
    cs = np.arange(nb_pad)[None, :] * CMP_STRIDE
    ss = np.arange(ns_pad)[:, None] * SEL_BLOCK
    ov = (cs < ss + SEL_BLOCK) & (cs + CMP_BLOCK > ss) & (np.arange(ns_pad)[:, None] < ns)
    return jnp.asarray(ov, BF16)


def _prompt_layer(x, mem, lw, tabs):
    b, t, d = x.shape
    ns = -(-t // SEL_BLOCK)
    qT, kvT, kc, krm, ngT, a, c, mg = _in_proj(x, lw['norm_g'][0], lw['w_all'], lw['wt_all'], 256)
    ocT, selT = _cmp_select(kc, qT, ngT, lw['cmp'], tabs['biasc'], tabs['ovT'], ns)
    onT = _selwin(tabs['rb_flat'], qT, krm, kvT, selT, ngT, ocT, tabs['tb'], tabs['ex'])
    y1, sta, stc = _mix(onT, a, c, mg, x, lw['w_nsa_out'], lw['w_a_out'], lw['w_c_out'], lw['w_o'],
                        lw['conv_a_w'], lw['conv_c_w'], lw['conv_c_b'], lw['conv_c_ln_g'], lw['conv_c_ln_b'],
                        lw['norm_g'][1], 256)
    n_mem = mem.shape[1]
    mkv, mkv_bf = _norm_mm(mem.reshape(b * n_mem, d), lw['mem_norm_g'], lw['w_xkv'], 256, with_bf16=True)
    y2 = _xattn(y1, lw['norm_g'][2], lw['w_xq'], mkv_bf.reshape(b, n_mem, -1), lw['w_xo'], lw['norm_g'][3], 256)
    y3 = _ffn(y2.reshape(b * t, d), lw['norm_g'][4], lw['w_ffn_in'], lw['w_ffn_out'], lw['norm_g'][5], 256)
    n_win = min(WINDOW, t)
    kv_rows = kvT[:, :4 * GD, :].reshape(b, 4, N_KV, HEAD_DIM, t).transpose(0, 4, 1, 2, 3)
    win_rows = kvT[:, 4 * GD:, t - n_win:].reshape(b, 2, N_KV, HEAD_DIM, n_win).transpose(0, 4, 1, 2, 3)
    st = (kv_rows, win_rows, sta[:, 8 - (CONV_A - 1):], stc[:, 32 - (CONV_C - 1):],
          mkv.reshape(b, n_mem, 2, X_HEADS, X_HEAD_DIM))
    return y3.reshape(b, t, d), st


def _prompt_tables(rel_bias, t):
    nch = t // CMP_STRIDE
    ns = -(-t // SEL_BLOCK)
    nsp = -(-ns // 8) * 8
    rb_flat = rel_bias.reshape(-1)
    tpos = np.arange(t)[None, :]
    dist_c = tpos - (np.arange(nch)[:, None] * CMP_STRIDE + CMP_BLOCK - 1)
    biasc = _bias_table(rb_flat, dist_c)
    tl = np.arange(TQ)[None, :]
    sl = np.arange(TQ)[:, None]
    d0 = tl - sl
    d1 = TQ + tl - sl
    d2 = 2 * TQ + tl - sl
    d2 = np.where(d2 < WINDOW, d2, -1)
    tb = _bias_table(rb_flat, np.concatenate([d0, d1, d2], axis=0)).reshape(N_HEADS, 3, TQ, TQ)
    ex = jnp.asarray((np.arange(t)[:, None] // SEL_BLOCK) == np.arange(nsp)[None, :], BF16)
    return {'rb_flat': rb_flat, 'biasc': biasc, 'tb': tb, 'ex': ex, 'ovT': _overlap_T(nch, ns, nsp)}


def _sample_tables(rel_bias, cache_kv, page_table, cache_win, state_conv_a, state_conv_c, cache_mem):
    depth, n_pool, psz = cache_kv.shape[:3]
    nseq, n_pages = page_table.shape
    past = n_pages * psz
    assert past % SEL_BLOCK == 0 and psz % SEL_BLOCK == 0 and psz % CMP_STRIDE == 0
    nb = past // CMP_STRIDE
    ns = past // SEL_BLOCK + 1
    nsp = -(-ns // 128) * 128
    dist_c = past - (np.arange(nb)[None, :] * CMP_STRIDE + CMP_BLOCK - 1)
    bias_c = _bias_table(rel_bias.reshape(-1), dist_c).reshape(N_HEADS, nb)
    pps = math.gcd(n_pages, 16)
    return {
        'past': past, 'pps': pps, 'page_table': page_table, 'bias_c': bias_c,
        'ov': _overlap_T(nb, ns, nsp).T,
        'rbT': rel_bias.T.reshape(N_KV, HPG, N_BUCKETS),
        'cache_t': cache_kv.transpose(0, 1, 3, 4, 5, 2).reshape(depth, n_pool, 4 * GD, psz),
        'win_t': cache_win.transpose(0, 1, 3, 4, 5, 2),
        'state_a': state_conv_a,
        'state_c_t': state_conv_c.transpose(0, 2, 1, 3),
        'mem': cache_mem.reshape(depth, nseq, cache_mem.shape[2], -1),
    }


def _layer_weights(l, d, norm_g, mem_norm_g, w_in, conv_a_w, conv_c_w, conv_c_b, conv_c_ln_g, conv_c_ln_b, cmp_w1,
                   cmp_b1, cmp_w2, cmp_b2, cmp_pe, w_nsa_out, w_a_out, w_c_out, w_o, w_xq, w_xkv, w_xo, w_ffn_in,
                   w_ffn_out):
    perm = _in_proj_perm(d)
    n_all = _in_proj_cols(d)[3]
    w_all = jnp.pad(w_in[l][:, perm], ((0, 0), (0, n_all - perm.shape[0]))).astype(BF16)
    bf = lambda w: w[l].astype(BF16)
    return {
        'norm_g': norm_g[l], 'mem_norm_g': mem_norm_g[l], 'w_all': w_all, 'wt_all': w_all.T,
        'cmp': _prep_compress(cmp_w1[l], cmp_b1[l], cmp_w2[l], cmp_b2[l], cmp_pe[l]),
        'conv_a_w': conv_a_w[l], 'conv_c_w': conv_c_w[l], 'conv_c_b': conv_c_b[l],
        'conv_c_ln_g': conv_c_ln_g[l], 'conv_c_ln_b': conv_c_ln_b[l],
        'w_nsa_out': bf(w_nsa_out), 'w_a_out': bf(w_a_out), 'w_c_out': bf(w_c_out), 'w_o': bf(w_o),
        'w_xq': bf(w_xq), 'w_xkv': bf(w_xkv), 'w_xo': bf(w_xo), 'w_ffn_in': bf(w_ffn_in), 'w_ffn_out': bf(w_ffn_out),
    }


def kernel(x_prompt, x_sample, mem_prompt, cache_kv, page_table, cache_win, state_conv_a, state_conv_c, cache_mem, rel_bias, norm_g, mem_norm_g, w_in, conv_a_w, conv_c_w, conv_c_b, conv_c_ln_g, conv_c_ln_b, cmp_w1, cmp_b1, cmp_w2, cmp_b2, cmp_pe, w_nsa_out, w_a_out, w_c_out, w_o, w_xq, w_xkv, w_xo, w_ffn_in, w_ffn_out):
    depth = w_in.shape[0]
    b, t, d = x_prompt.shape
    assert t % TQ == 0 and WINDOW == 2 * TQ and t >= WINDOW
    tabs = _prompt_tables(rel_bias, t)
    dec = _sample_tables(rel_bias, cache_kv, page_table, cache_win, state_conv_a, state_conv_c, cache_mem)
    yp = x_prompt
    ys = x_sample.reshape(x_sample.shape[0], d)
    outs_p, outs_s = [], []
    for l in range(depth):
        lw = _layer_weights(l, d, norm_g, mem_norm_g, w_in, conv_a_w, conv_c_w, conv_c_b, conv_c_ln_g, conv_c_ln_b,
                            cmp_w1, cmp_b1, cmp_w2, cmp_b2, cmp_pe, w_nsa_out, w_a_out, w_c_out, w_o, w_xq, w_xkv,
                            w_xo, w_ffn_in, w_ffn_out)
        yp, stp = _prompt_layer(yp, mem_prompt, lw, tabs)
        ys, sts = _sample_layer(ys, l, lw, dec)
        outs_p.append(stp)
        outs_s.append(sts)
    sp = lambda i: jnp.stack([o[i] for o in outs_p])
    ss = lambda i: jnp.stack([o[i] for o in outs_s])
    return (yp, ys.reshape(x_sample.shape), sp(0), ss(0), sp(1), ss(1), sp(2), ss(2), sp(3), ss(3), sp(4))
```
